```python
import jax
import jax.numpy as jnp
from jax import lax
import numpy as np

D_MODEL = 1024
BATCH = 16
SEQ = 256
DEPTH = 2
DEC_BATCH = 2
DEC_SEQ = 4096
PAST_LEN = 512

GRID_W = 64
GD_H = 4
GD_DK = 128
GD_DV = 128
GD_CHUNK = 64
CONV_K = 3
ROPE_BASE = 10000.0
NA_H = 8
NA_D = 64
NA_WR = 8
NA_WC = 16
NA_QC = 16
NA_KC = 32
Q_BLOCK = 128
HG_H = 4
HG_DK = 128
HG_DV = 128
HG_CHUNK = 32
RW_H = 8
RW_D = 64
RW_DECAY_LORA = 64
RW_AAA_LORA = 64
RW_GATE_LORA = 128
RW_GN_EPS = 64e-5
RW_IN_W = 3 * RW_H * RW_D + 2 * RW_DECAY_LORA + RW_AAA_LORA + RW_GATE_LORA
N_BRANCH = 4
BRANCH_W = 512
PEER_HEADS = 8
PEER_NKEYS = 128
PEER_DK = 256
PEER_TOPK = 16
PEER_BLOCK = 128
N_EXPERTS = PEER_NKEYS * PEER_NKEYS
LN_EPS = 1e-5
NORM_EPS = 1e-6

IN_LAYOUT = (
    ('gdn_qkv', 2 * GD_H * GD_DK + GD_H * GD_DV),
    ('gdn_g', GD_H * GD_DV),
    ('gdn_a', 2 * GD_H),
    ('gdn_b', 2 * GD_H),
    ('na_q', NA_H * NA_D),
    ('na_k', NA_H * NA_D),
    ('na_v', NA_H * NA_D),
    ('hg_q', HG_H * HG_DK),
    ('hg_f', 2 * HG_H * HG_DK),
    ('hg_i', HG_H * HG_DV),
    ('hg_g', HG_H * HG_DV),
    ('rw', RW_IN_W),
    ('merge', N_BRANCH * D_MODEL),
)
IN_W = sum(w for _, w in IN_LAYOUT)

kernel_name = 'hybrid_flow_backbone_step'


def layer_norm(x, g, b):
    xf = x.astype(jnp.float32)
    mu = jnp.mean(xf, -1, keepdims=True)
    var = jnp.mean(jnp.square(xf - mu), -1, keepdims=True)
    return ((xf - mu) * lax.rsqrt(var + LN_EPS) * g + b).astype(x.dtype)


def rms_norm(x, w):
    xf = x.astype(jnp.float32)
    return xf * lax.rsqrt(jnp.mean(xf * xf, -1, keepdims=True) + NORM_EPS) * w


def l2_normalize(x):
    xf = x.astype(jnp.float32)
    return xf * lax.rsqrt(jnp.sum(xf * xf, -1, keepdims=True) + NORM_EPS)


def to_heads(t, n):
    B, L, _ = t.shape
    return jnp.moveaxis(t.reshape(B, L, n, -1), 2, 1)


def from_heads(t):
    B, H, L, d = t.shape
    return jnp.moveaxis(t, 1, 2).reshape(B, L, H * d)


def flip_seq(t):
    return jnp.flip(t, axis=2)


def split_in(p):
    out, off = {}, 0
    for name, w in IN_LAYOUT:
        out[name] = p[..., off:off + w]
        off += w
    return out


def centred_dwconv(x, w):
    pad = CONV_K // 2
    L = x.shape[1]
    xp = jnp.pad(x, ((0, 0), (pad, pad), (0, 0)))
    return sum(xp[:, j:j + L] * w[j] for j in range(CONV_K))


def centred_shift(x):
    zero = jnp.zeros_like(x[:, :1])
    return jnp.concatenate([zero, x[:, :-1]], 1), jnp.concatenate([x[:, 1:], zero], 1)


def axial_rope(x):
    L, dh = x.shape[-2], x.shape[-1]
    half = dh // 2
    nf = half // 2
    inv = ROPE_BASE ** (-jnp.arange(nf, dtype=jnp.float32) / nf)
    t = jnp.arange(L)

    def rot(xa, pos):
        ang = pos.astype(jnp.float32)[:, None] * inv[None, :]
        cos, sin = jnp.cos(ang), jnp.sin(ang)
        x1, x2 = xa[..., :nf], xa[..., nf:]
        return jnp.concatenate([x1 * cos - x2 * sin, x1 * sin + x2 * cos], -1)

    return jnp.concatenate([rot(x[..., :half], t // GRID_W), rot(x[..., half:], t % GRID_W)], -1)


def gdn_chunk_scan(q, k, v, log_a, beta, s0):
    B, H, L, dk = q.shape
    dv = v.shape[-1]
    C = GD_CHUNK
    n = L // C
    q = q.reshape(B, H, n, C, dk)
    k = k.reshape(B, H, n, C, dk)
    v = v.reshape(B, H, n, C, dv)
    beta = beta.reshape(B, H, n, C, 1)
    g = jnp.cumsum(log_a.reshape(B, H, n, C), axis=-1)
    incl = jnp.tril(jnp.ones((C, C), bool))
    strict = jnp.tril(jnp.ones((C, C), bool), -1)
    decay = jnp.exp(jnp.where(incl, g[..., :, None] - g[..., None, :], -jnp.inf))
    a_mat = jnp.where(strict, beta * jnp.einsum('bhncd,bhnjd->bhncj', k, k) * decay, 0.0)
    eye = jnp.eye(C, dtype=jnp.float32)
    rhs = jnp.concatenate([beta * v, beta * jnp.exp(g)[..., None] * k], -1)
    sol = lax.linalg.triangular_solve(eye + a_mat, rhs, left_side=True, lower=True, unit_diagonal=True)
    u, w = sol[..., :dv], sol[..., dv:]
    attn = jnp.einsum('bhncd,bhnjd->bhncj', q, k) * decay
    q_dec = q * jnp.exp(g)[..., None]
    g_last = g[..., -1]
    k_dec = k * jnp.exp(g_last[..., None] - g)[..., None]

    def step(S, xs):
        u_c, w_c, attn_c, qd_c, kd_c, gl_c = xs
        v_new = u_c - jnp.einsum('bhcd,bhde->bhce', w_c, S)
        o = jnp.einsum('bhcd,bhde->bhce', qd_c, S) + jnp.einsum('bhcj,bhje->bhce', attn_c, v_new)
        S = S * jnp.exp(gl_c)[..., None, None] + jnp.einsum('bhcd,bhce->bhde', kd_c, v_new)
        return S, o

    mv = lambda t: jnp.moveaxis(t, 2, 0)
    S, o = lax.scan(step, s0.astype(jnp.float32), (mv(u), mv(w), mv(attn), mv(q_dec), mv(k_dec), mv(g_last)))
    return jnp.moveaxis(o, 0, 2).reshape(B, H, L, dv), S


def hgrn_chunk_scan(q, kf, logf, i, s0):
    B, H, L, dk = q.shape
    dv = i.shape[-1]
    C = HG_CHUNK
    n = L // C
    chunks = lambda t: jnp.moveaxis(t.reshape(B, H, n, C, t.shape[-1]), 2, 0)
    b = jnp.cumsum(logf.reshape(B, H, n, C, dk), axis=3)
    incl = jnp.tril(jnp.ones((C, C), bool))[:, :, None]

    def step(S, xs):
        q_c, k_c, b_c, i_c = xs
        dec = jnp.exp(jnp.where(incl, b_c[:, :, :, None, :] - b_c[:, :, None, :, :], -jnp.inf))
        attn = jnp.einsum('bhtc,bhjc,bhtjc->bhtj', q_c, k_c, dec)
        b_last = b_c[:, :, -1:, :]
        o = jnp.einsum('bhtc,bhce->bhte', q_c * jnp.exp(b_c), S) + jnp.einsum('bhtj,bhje->bhte', attn, i_c)
        S = jnp.exp(b_last[:, :, 0, :, None]) * S + jnp.einsum('bhjc,bhje->bhce', k_c * jnp.exp(b_last - b_c), i_c)
        return S, o

    S, o = lax.scan(step, s0.astype(jnp.float32), (chunks(q), chunks(kf), jnp.moveaxis(b, 2, 0), chunks(i)))
    return jnp.moveaxis(o, 0, 2).reshape(B, H, L, dv), S


def rwkv_scan(r, dec, k, v, kk, a, s0, reverse):
    def step(S, xs):
        r_t, w_t, k_t, v_t, kk_t, a_t = xs
        sa = jnp.einsum('bhvk,bhk->bhv', S, -kk_t)
        S = S * w_t[:, :, None, :] + sa[..., None] * (kk_t * a_t)[:, :, None, :] + v_t[..., None] * k_t[:, :, None, :]
        return S, jnp.einsum('bhvk,bhk->bhv', S, r_t)

    xs = tuple(jnp.moveaxis(t, 1, 0) for t in (r, dec, k, v, kk, a))
    S, y = lax.scan(step, s0.astype(jnp.float32), xs, reverse=reverse)
    return jnp.moveaxis(y, 0, 1), S


def na_context(q, k, v):
    B, H, L, d = q.shape
    nb = L // Q_BLOCK
    qb = jnp.moveaxis(q.reshape(B, H, nb, Q_BLOCK, d), 2, 0)

    def block(qi):
        s = jnp.einsum('bhqd,bhkd->bhqk', qi, k).astype(jnp.float32)
        p = jax.nn.softmax(s, -1).astype(v.dtype)
        return jnp.einsum('bhqk,bhkd->bhqd', p, v)

    o = lax.map(block, qb)
    return jnp.moveaxis(o, 0, 2).reshape(B, H, L, d)


def na_latent(q, k, v, k_ctx, v_ctx, rpb):
    B, H, L, d = q.shape
    rows = L // GRID_W
    wr = min(NA_WR, rows)
    ncb = GRID_W // NA_QC
    qcol = np.arange(GRID_W).reshape(ncb, NA_QC)
    kc0 = np.clip(np.arange(ncb) * NA_QC - NA_WC // 2, 0, GRID_W - NA_KC)
    kcol = kc0[:, None] + np.arange(NA_KC)
    qs = np.clip(qcol - NA_WC // 2, 0, GRID_W - NA_WC)
    col_ok = (kcol[:, None, :] >= qs[..., None]) & (kcol[:, None, :] < qs[..., None] + NA_WC)
    col_bias = np.clip(kcol[:, None, :] - qcol[..., None] + NA_WC - 1, 0, 2 * NA_WC - 2)
    kg = k.reshape(B, H, rows, GRID_W, d)[:, :, :, kcol]
    vg = v.reshape(B, H, rows, GRID_W, d)[:, :, :, kcol]
    qr = jnp.moveaxis(q.reshape(B, H, rows, ncb, NA_QC, d), 2, 0)
    rpb = rpb.astype(jnp.float32)
    nwin = wr * NA_KC

    def row(xs):
        q_r, r = xs
        s0 = jnp.clip(r - wr // 2, 0, rows - wr)
        k_b = lax.dynamic_slice_in_dim(kg, s0, wr, axis=2)
        v_b = lax.dynamic_slice_in_dim(vg, s0, wr, axis=2)
        s_win = jnp.einsum('bhnqd,bhwnkd->bhnqwk', q_r, k_b).astype(jnp.float32)
        roff = s0 + jnp.arange(wr) - r + NA_WR - 1
        bias = jnp.transpose(rpb[:, roff][:, :, col_bias], (0, 2, 3, 1, 4))
        s_win = jnp.where(col_ok[:, :, None, :], s_win + bias, -jnp.inf)
        s_ctx = jnp.einsum('bhnqd,bhkd->bhnqk', q_r, k_ctx).astype(jnp.float32)
        s = jnp.concatenate([s_win.reshape(B, H, ncb, NA_QC, nwin), s_ctx], -1)
        p = jax.nn.softmax(s, -1).astype(v.dtype)
        p_win = p[..., :nwin].reshape(B, H, ncb, NA_QC, wr, NA_KC)
        return (jnp.einsum('bhnqwk,bhwnkd->bhnqd', p_win, v_b)
                + jnp.einsum('bhnqk,bhkd->bhnqd', p[..., nwin:], v_ctx))

    o = lax.map(row, (qr, jnp.arange(rows)))
    return jnp.moveaxis(o, 0, 2).reshape(B, H, L, d)


def token_mixing(h, lp, lb, cache):
    latent = cache is not None
    B, L, _ = h.shape
    dt = h.dtype
    f32 = jnp.float32
    P = split_in(h @ lp['w_in'])
    if latent:
        s_gdn, s_hg, s_rw, k_ctx, v_ctx = cache
    else:
        s_gdn = jnp.zeros((B, 2, GD_H, GD_DK, GD_DV), f32)
        s_hg = jnp.zeros((B, 2, HG_H, HG_DK, HG_DV), f32)
        s_rw = jnp.zeros((B, 2, RW_H, RW_D, RW_D), f32)

    qkv = jax.nn.silu(centred_dwconv(P['gdn_qkv'], lp['gdn_conv']))
    q, k, v = jnp.split(qkv, [GD_H * GD_DK, 2 * GD_H * GD_DK], axis=-1)
    q = l2_normalize(to_heads(q, GD_H))
    k = l2_normalize(to_heads(k, GD_H))
    if latent:
        q, k = axial_rope(q), axial_rope(k)
    q = q * GD_DK ** -0.5
    v = to_heads(v, GD_H).astype(f32)
    a_in = P['gdn_a'].astype(f32).reshape(B, L, 2, GD_H)
    log_a = jnp.moveaxis(-jnp.exp(lp['gdn_A_log'].astype(f32)) * jax.nn.softplus(a_in + lp['gdn_dt_bias']), 1, -1)
    beta = jnp.moveaxis(jax.nn.sigmoid(P['gdn_b'].astype(f32).reshape(B, L, 2, GD_H)), 1, -1)
    o_f, sg_f = gdn_chunk_scan(q, k, v, log_a[:, 0], beta[:, 0], s_gdn[:, 0])
    o_b, sg_b = gdn_chunk_scan(flip_seq(q), flip_seq(k), flip_seq(v), flip_seq(log_a[:, 1]),
                               flip_seq(beta[:, 1]), s_gdn[:, 1])
    o = o_f + flip_seq(o_b)
    y_a = from_heads(rms_norm(o, lp['gdn_norm']) * jax.nn.silu(to_heads(P['gdn_g'], GD_H).astype(f32))).astype(dt)

    qn = to_heads(P['na_q'], NA_H) * NA_D ** -0.5
    kn = to_heads(P['na_k'], NA_H)
    vn = to_heads(P['na_v'], NA_H)
    on = na_latent(qn, kn, vn, k_ctx, v_ctx, lp['na_rpb']) if latent else na_context(qn, kn, vn)
    y_b = from_heads(on).astype(dt)

    qh = to_heads(P['hg_q'], HG_H).astype(f32)
    ih = to_heads(P['hg_i'], HG_H).astype(f32)
    z = jnp.transpose(P['hg_f'].astype(f32).reshape(B, L, 2, HG_H, HG_DK), (0, 2, 3, 1, 4))
    lbh = lb.reshape(HG_H, 1, HG_DK)
    logf = jnp.logaddexp(jnp.log(lbh), jnp.log1p(-lbh) + jax.nn.log_sigmoid(z))
    kf = (1.0 - lbh) * jax.nn.sigmoid(-z)
    o_f, sh_f = hgrn_chunk_scan(qh, kf[:, 0], logf[:, 0], ih, s_hg[:, 0])
    o_b, sh_b = hgrn_chunk_scan(flip_seq(qh), flip_seq(kf[:, 1]), flip_seq(logf[:, 1]), flip_seq(ih), s_hg[:, 1])
    o = o_f + flip_seq(o_b)
    y_c = from_heads(rms_norm(o, lp['hg_norm']) * jax.nn.silu(to_heads(P['hg_g'], HG_H).astype(f32))).astype(dt)

    xr = P['rw']
    prev, nxt = centred_shift(xr)
    xr = (xr + lp['rw_mu'][0] * (prev - xr) + lp['rw_mu'][1] * (nxt - xr)).astype(f32)
    RW = RW_H * RW_D
    r, kr, vr = xr[..., :RW], xr[..., RW:2 * RW], xr[..., 2 * RW:3 * RW]
    o0 = 3 * RW
    wl = xr[..., o0:o0 + 2 * RW_DECAY_LORA]
    o0 += 2 * RW_DECAY_LORA
    al = xr[..., o0:o0 + RW_AAA_LORA]
    o0 += RW_AAA_LORA
    gl = xr[..., o0:o0 + RW_GATE_LORA]
    a = jax.nn.sigmoid(lp['rw_a0'] + al @ lp['rw_a2'])
    gate = jax.nn.sigmoid(gl) @ lp['rw_g2']
    sp = lambda t: t.reshape(B, L, RW_H, RW_D)
    kk = l2_normalize(sp(kr * lp['rw_kk']))
    kr = kr * (1.0 + (a - 1.0) * lp['rw_ka'])
    r4, k4, v4, a4 = sp(r), sp(kr), sp(vr), sp(a)

    def decay(dirn):
        wd = wl[..., dirn * RW_DECAY_LORA:(dirn + 1) * RW_DECAY_LORA]
        wlog = -jax.nn.softplus(-(lp['rw_w0'][dirn] + jnp.tanh(wd) @ lp['rw_w2'][dirn])) - 0.5
        return sp(jnp.exp(-jnp.exp(wlog)))

    yr_f, sr_f = rwkv_scan(r4, decay(0), k4, v4, kk, a4, s_rw[:, 0], False)
    yr_b, sr_b = rwkv_scan(r4, decay(1), k4, v4, kk, a4, s_rw[:, 1], True)
    y = yr_f + yr_b
    mu = jnp.mean(y, -1, keepdims=True)
    var = jnp.mean(jnp.square(y - mu), -1, keepdims=True)
    y = ((y - mu) * lax.rsqrt(var + RW_GN_EPS)).reshape(B, L, RW) * lp['rw_gn_w'] + lp['rw_gn_b']
    y = y + (jnp.sum(r4 * k4 * lp['rw_rk'], -1, keepdims=True) * v4).reshape(B, L, RW)
    y_d = (y * gate).astype(dt)

    gates = jax.nn.sigmoid(P['merge'].astype(f32)).reshape(B, L, N_BRANCH, D_MODEL)
    branches = jnp.stack([y_a, y_b, y_c, y_d], axis=2)
    proj = jnp.einsum('blmc,mcd->blmd', branches, lp['w_branch'])
    out = jnp.sum(gates * proj, axis=2).astype(dt) @ lp['w_out']
    if latent:
        return out, None
    return out, (jnp.stack([sg_f, sg_b], 1), jnp.stack([sh_f, sh_b], 1), jnp.stack([sr_f, sr_b], 1), kn, vn)


def peer(h, wq, keys, u_tab, v_tab):
    B, L, D = h.shape
    xt = h.reshape(-1, PEER_BLOCK, D)
    half = PEER_DK // 2
    keys = keys.astype(jnp.float32)

    def block(xb):
        t = xb.shape[0]
        qh = (xb @ wq).reshape(t, PEER_HEADS, 2, half).astype(jnp.float32)
        s = jnp.einsum('thpd,hpnd->thpn', qh, keys)
        sv, si = lax.top_k(s, PEER_TOPK)
        cand = (sv[:, :, 0, :, None] + sv[:, :, 1, None, :]).reshape(t, PEER_HEADS, -1)
        cidx = (si[:, :, 0, :, None] * PEER_NKEYS + si[:, :, 1, None, :]).reshape(t, PEER_HEADS, -1)
        best, pos = lax.top_k(cand, PEER_TOPK)
        e = jnp.take_along_axis(cidx, pos, -1)
        g = jax.nn.softmax(best, -1)
        act = jax.nn.gelu(jnp.einsum('thkd,td->thk', u_tab[e], xb).astype(jnp.float32))
        return jnp.einsum('thk,thkd->td', (g * act).astype(xb.dtype), v_tab[e])

    return lax.map(block, xt).reshape(B, L, D)


def trunk_layer(x, mod, lp, lb, cache, alpha):
    sh1, sc1, g1, sh2, sc2, g2 = mod
    h = x * (1 + sc1) + sh1
    y, ctx_out = token_mixing(h, lp, lb, cache)
    x = layer_norm(alpha * x + g1 * y, lp['ln1_g'], lp['ln1_b'])
    h = x * (1 + sc2) + sh2
    y = peer(h, lp['peer_wq'], lp['peer_keys'], lp['peer_u'], lp['peer_v'])
    x = layer_norm(alpha * x + g2 * y, lp['ln2_g'], lp['ln2_b'])
    return x, ctx_out


def setup_inputs(seed: int = 0) -> dict:
    key = jax.random.key(seed)
    ks = iter(jax.random.split(key, 48))
    f32 = jnp.float32
    D = D_MODEL
    RW = RW_H * RW_D
    beta_dn = (8.0 * DEPTH) ** -0.25

    def nrm(shape, scale=1.0):
        return scale * jax.random.normal(next(ks), shape, f32)

    def unif(shape, lo, hi):
        return jax.random.uniform(next(ks), shape, f32, lo, hi)

    return {
        'x_prompt': nrm((BATCH, SEQ, D)),
        'x_sample': nrm((DEC_BATCH, DEC_SEQ, D)),
        'c': nrm((DEC_BATCH, D)),
        'c_ctx': nrm((D,)),
        'state_gdn': nrm((DEC_BATCH, DEPTH, 2, GD_H, GD_DK, GD_DV), 0.1),
        'state_hgrn': nrm((DEC_BATCH, DEPTH, 2, HG_H, HG_DK, HG_DV), 0.3),
        'state_rwkv': nrm((DEC_BATCH, DEPTH, 2, RW_H, RW_D, RW_D), 0.3),
        'cache_k_na': nrm((DEC_BATCH, DEPTH, NA_H, PAST_LEN, NA_D)),
        'cache_v_na': nrm((DEC_BATCH, DEPTH, NA_H, PAST_LEN, NA_D)),
        'w_mod': nrm((DEPTH, D, 6 * D), 0.5 * D ** -0.5),
        'b_mod': nrm((DEPTH, 6 * D), 0.02),
        'w_in': nrm((DEPTH, D, IN_W), D ** -0.5),
        'gdn_conv': nrm((DEPTH, CONV_K, 2 * GD_H * GD_DK + GD_H * GD_DV), CONV_K ** -0.5),
        'gdn_A_log': jnp.log(unif((DEPTH, 2, GD_H), 1.0, 16.0)),
        'gdn_dt_bias': jnp.log(jnp.expm1(unif((DEPTH, 2, GD_H), 0.001, 0.1))),
        'gdn_norm': 1.0 + nrm((DEPTH, GD_DV), 0.1),
        'na_rpb': nrm((DEPTH, NA_H, 2 * NA_WR - 1, 2 * NA_WC - 1), 0.1),
        'hg_lb_logits': nrm((DEPTH, HG_H * HG_DK)),
        'hg_norm': 1.0 + nrm((DEPTH, HG_DV), 0.1),
        'rw_mu': unif((DEPTH, 2, RW_IN_W), 0.0, 0.5),
        'rw_w0': nrm((DEPTH, 2, RW), 0.5),
        'rw_w2': nrm((DEPTH, 2, RW_DECAY_LORA, RW), 0.1),
        'rw_a0': nrm((DEPTH, RW), 0.5),
        'rw_a2': nrm((DEPTH, RW_AAA_LORA, RW), 0.1),
        'rw_g2': nrm((DEPTH, RW_GATE_LORA, RW), RW_GATE_LORA ** -0.5),
        'rw_kk': 1.0 + nrm((DEPTH, RW), 0.1),
        'rw_ka': 1.0 + nrm((DEPTH, RW), 0.1),
        'rw_rk': nrm((DEPTH, RW_H, RW_D), 0.1),
        'rw_gn_w': 1.0 + nrm((DEPTH, RW), 0.1),
        'rw_gn_b': nrm((DEPTH, RW), 0.02),
        'w_branch': nrm((DEPTH, N_BRANCH, BRANCH_W, D), BRANCH_W ** -0.5),
        'w_out': nrm((DEPTH, D, D), beta_dn * D ** -0.5),
        'ln1_g': 1.0 + nrm((DEPTH, D), 0.1),
        'ln1_b': nrm((DEPTH, D), 0.02),
        'ln2_g': 1.0 + nrm((DEPTH, D), 0.1),
        'ln2_b': nrm((DEPTH, D), 0.02),
        'peer_wq': nrm((DEPTH, D, PEER_HEADS * PEER_DK), D ** -0.5),
        'peer_keys': nrm((DEPTH, PEER_HEADS, 2, PEER_NKEYS, PEER_DK // 2), (PEER_DK // 2) ** -0.5),
        'peer_u': nrm((DEPTH, N_EXPERTS, D), D ** -0.5),
        'peer_v': nrm((DEPTH, N_EXPERTS, D), beta_dn),
    }


def reference(x_prompt, x_sample, c, c_ctx, state_gdn, state_hgrn, state_rwkv, cache_k_na, cache_v_na,
              w_mod, b_mod, w_in, gdn_conv, gdn_A_log, gdn_dt_bias, gdn_norm, na_rpb, hg_lb_logits, hg_norm,
              rw_mu, rw_w0, rw_w2, rw_a0, rw_a2, rw_g2, rw_kk, rw_ka, rw_rk, rw_gn_w, rw_gn_b,
              w_branch, w_out, ln1_g, ln1_b, ln2_g, ln2_b, peer_wq, peer_keys, peer_u, peer_v):
    alpha = (2.0 * DEPTH) ** 0.25
    lb_all = jnp.cumsum(jax.nn.softmax(hg_lb_logits.astype(jnp.float32), axis=0), axis=0)
    lb_all = lb_all - lb_all[0]
    y_prompt, y_sample = x_prompt, x_sample
    gdn_l, hg_l, rw_l, k_l, v_l = [], [], [], [], []
    for l in range(DEPTH):
        lp = {
            'w_in': w_in[l], 'gdn_conv': gdn_conv[l], 'gdn_A_log': gdn_A_log[l], 'gdn_dt_bias': gdn_dt_bias[l],
            'gdn_norm': gdn_norm[l], 'na_rpb': na_rpb[l], 'hg_norm': hg_norm[l], 'rw_mu': rw_mu[l],
            'rw_w0': rw_w0[l], 'rw_w2': rw_w2[l], 'rw_a0': rw_a0[l], 'rw_a2': rw_a2[l], 'rw_g2': rw_g2[l],
            'rw_kk': rw_kk[l], 'rw_ka': rw_ka[l], 'rw_rk': rw_rk[l], 'rw_gn_w': rw_gn_w[l], 'rw_gn_b': rw_gn_b[l],
            'w_branch': w_branch[l], 'w_out': w_out[l], 'ln1_g': ln1_g[l], 'ln1_b': ln1_b[l],
            'ln2_g': ln2_g[l], 'ln2_b': ln2_b[l], 'peer_wq': peer_wq[l], 'peer_keys': peer_keys[l],
            'peer_u': peer_u[l], 'peer_v': peer_v[l],
        }
        mod_ctx = jnp.split(jax.nn.silu(c_ctx) @ w_mod[l] + b_mod[l], 6, axis=-1)
        mod_lat = [m[:, None, :] for m in jnp.split(jax.nn.silu(c) @ w_mod[l] + b_mod[l], 6, axis=-1)]
        y_prompt, ctx_out = trunk_layer(y_prompt, mod_ctx, lp, lb_all[l], None, alpha)
        cache = (state_gdn[:, l], state_hgrn[:, l], state_rwkv[:, l], cache_k_na[:, l], cache_v_na[:, l])
        y_sample, _ = trunk_layer(y_sample, mod_lat, lp, lb_all[l], cache, alpha)
        gdn_l.append(ctx_out[0])
        hg_l.append(ctx_out[1])
        rw_l.append(ctx_out[2])
        k_l.append(ctx_out[3])
        v_l.append(ctx_out[4])
    new_state_gdn = jnp.stack(gdn_l, 1)
    new_state_hgrn = jnp.stack(hg_l, 1)
    new_state_rwkv = jnp.stack(rw_l, 1)
    new_cache_k_na = jnp.stack(k_l, 1)
    new_cache_v_na = jnp.stack(v_l, 1)
    return (y_prompt, y_sample, new_state_gdn, new_state_hgrn, new_state_rwkv, new_cache_k_na, new_cache_v_na)
```

```python
import functools

import jax
import jax.numpy as jnp
import numpy as np
from jax import lax
from jax.experimental import pallas as pl
from jax.experimental.pallas import tpu as pltpu

D_MODEL = 1024
BATCH = 16
SEQ = 256
DEPTH = 2
DEC_BATCH = 2
DEC_SEQ = 4096
PAST_LEN = 512
GRID_W = 64
GD_H = 4
GD_DK = 128
GD_DV = 128
GD_CHUNK = 64
CONV_K = 3
ROPE_BASE = 10000.0
NA_H = 8
NA_D = 64
NA_WR = 8
NA_WC = 16
NA_QC = 16
NA_KC = 32
Q_BLOCK = 128
HG_H = 4
HG_DK = 128
HG_DV = 128
HG_CHUNK = 32
RW_H = 8
RW_D = 64
RW_DECAY_LORA = 64
RW_AAA_LORA = 64
RW_GATE_LORA = 128
RW_GN_EPS = 64e-5
RW_IN_W = 3 * RW_H * RW_D + 2 * RW_DECAY_LORA + RW_AAA_LORA + RW_GATE_LORA
N_BRANCH = 4
BRANCH_W = 512
PEER_HEADS = 8
PEER_NKEYS = 128
PEER_DK = 256
PEER_TOPK = 16
PEER_BLOCK = 128
N_EXPERTS = PEER_NKEYS * PEER_NKEYS
LN_EPS = 1e-5
NORM_EPS = 1e-6

IN_LAYOUT = (
    ('gdn_qkv', 2 * GD_H * GD_DK + GD_H * GD_DV),
    ('gdn_g', GD_H * GD_DV),
    ('gdn_a', 2 * GD_H),
    ('gdn_b', 2 * GD_H),
    ('na_q', NA_H * NA_D),
    ('na_k', NA_H * NA_D),
    ('na_v', NA_H * NA_D),
    ('hg_q', HG_H * HG_DK),
    ('hg_f', 2 * HG_H * HG_DK),
    ('hg_i', HG_H * HG_DV),
    ('hg_g', HG_H * HG_DV),
    ('rw', RW_IN_W),
    ('merge', N_BRANCH * D_MODEL),
)
IN_W = sum(w for _, w in IN_LAYOUT)

LANES = 128
SUBLANES = 8
VMEM_LIMIT_BYTES = 48 * 1024 * 1024


def _mm_body(x_ref, w_ref, o_ref):
    o_ref[...] = jnp.dot(x_ref[...].astype(jnp.bfloat16), w_ref[...].astype(jnp.bfloat16),
                         preferred_element_type=jnp.float32)


def _pick_tile(n, target):
    t = min(n, target)
    while n % t:
        t -= LANES
    return t


def matmul(x, w, tm=512, tn=1024):
    M, K = x.shape
    N = w.shape[1]
    n_pad = (-N) % LANES
    if n_pad:
        w = jnp.pad(w, ((0, 0), (0, n_pad)))
    Np = N + n_pad
    tm = _pick_tile(M, tm)
    tn = _pick_tile(Np, tn)
    out = pl.pallas_call(
        _mm_body,
        grid=(Np // tn, M // tm),
        in_specs=[pl.BlockSpec((tm, K), lambda j, i: (i, 0)),
                  pl.BlockSpec((K, tn), lambda j, i: (0, j))],
        out_specs=pl.BlockSpec((tm, tn), lambda j, i: (i, j)),
        out_shape=jax.ShapeDtypeStruct((M, Np), jnp.float32),
        compiler_params=pltpu.CompilerParams(dimension_semantics=("arbitrary", "arbitrary"),
                                             vmem_limit_bytes=VMEM_LIMIT_BYTES),
        name="dense_proj",
    )(x, w)
    return out[:, :N] if n_pad else out


def mm3(x, w):
    B, L, K = x.shape
    return matmul(x.reshape(B * L, K), w).reshape(B, L, w.shape[1])


def layer_norm(x, g, b):
    xf = x.astype(jnp.float32)
    mu = jnp.mean(xf, -1, keepdims=True)
    var = jnp.mean(jnp.square(xf - mu), -1, keepdims=True)
    return ((xf - mu) * lax.rsqrt(var + LN_EPS) * g + b).astype(x.dtype)


def rms_norm(x, w):
    xf = x.astype(jnp.float32)
    return xf * lax.rsqrt(jnp.mean(xf * xf, -1, keepdims=True) + NORM_EPS) * w


def l2_normalize(x):
    xf = x.astype(jnp.float32)
    return xf * lax.rsqrt(jnp.sum(xf * xf, -1, keepdims=True) + NORM_EPS)


def to_heads(t, n):
    B, L, _ = t.shape
    return jnp.moveaxis(t.reshape(B, L, n, -1), 2, 1)


def from_heads(t):
    B, H, L, d = t.shape
    return jnp.moveaxis(t, 1, 2).reshape(B, L, H * d)


def flip_seq(t):
    return jnp.flip(t, axis=2)


def split_in(p):
    out, off = {}, 0
    for name, w in IN_LAYOUT:
        out[name] = p[..., off:off + w]
        off += w
    return out


def centred_dwconv(x, w):
    pad = CONV_K // 2
    L = x.shape[1]
    xp = jnp.pad(x, ((0, 0), (pad, pad), (0, 0)))
    return sum(xp[:, j:j + L] * w[j] for j in range(CONV_K))


def centred_shift(x):
    zero = jnp.zeros_like(x[:, :1])
    return jnp.concatenate([zero, x[:, :-1]], 1), jnp.concatenate([x[:, 1:], zero], 1)


def axial_rope(x):
    L, dh = x.shape[-2], x.shape[-1]
    half = dh // 2
    nf = half // 2
    inv = ROPE_BASE ** (-jnp.arange(nf, dtype=jnp.float32) / nf)
    t = jnp.arange(L)

    def rot(xa, pos):
        ang = pos.astype(jnp.float32)[:, None] * inv[None, :]
        cos, sin = jnp.cos(ang), jnp.sin(ang)
        x1, x2 = xa[..., :nf], xa[..., nf:]
        return jnp.concatenate([x1 * cos - x2 * sin, x1 * sin + x2 * cos], -1)

    return jnp.concatenate([rot(x[..., :half], t // GRID_W), rot(x[..., half:], t % GRID_W)], -1)


def gdn_chunk_scan(q, k, v, log_a, beta, s0):
    B, H, L, dk = q.shape
    dv = v.shape[-1]
    C = GD_CHUNK
    n = L // C
    q = q.reshape(B, H, n, C, dk)
    k = k.reshape(B, H, n, C, dk)
    v = v.reshape(B, H, n, C, dv)
    beta = beta.reshape(B, H, n, C, 1)
    g = jnp.cumsum(log_a.reshape(B, H, n, C), axis=-1)
    incl = jnp.tril(jnp.ones((C, C), bool))
    strict = jnp.tril(jnp.ones((C, C), bool), -1)
    decay = jnp.exp(jnp.where(incl, g[..., :, None] - g[..., None, :], -jnp.inf))
    a_mat = jnp.where(strict, beta * jnp.einsum('bhncd,bhnjd->bhncj', k, k) * decay, 0.0)
    eye = jnp.eye(C, dtype=jnp.float32)
    rhs = jnp.concatenate([beta * v, beta * jnp.exp(g)[..., None] * k], -1)
    sol = lax.linalg.triangular_solve(eye + a_mat, rhs, left_side=True, lower=True, unit_diagonal=True)
    u, w = sol[..., :dv], sol[..., dv:]
    attn = jnp.einsum('bhncd,bhnjd->bhncj', q, k) * decay
    q_dec = q * jnp.exp(g)[..., None]
    g_last = g[..., -1]
    k_dec = k * jnp.exp(g_last[..., None] - g)[..., None]

    def step(S, xs):
        u_c, w_c, attn_c, qd_c, kd_c, gl_c = xs
        v_new = u_c - jnp.einsum('bhcd,bhde->bhce', w_c, S)
        o = jnp.einsum('bhcd,bhde->bhce', qd_c, S) + jnp.einsum('bhcj,bhje->bhce', attn_c, v_new)
        S = S * jnp.exp(gl_c)[..., None, None] + jnp.einsum('bhcd,bhce->bhde', kd_c, v_new)
        return S, o

    mv = lambda t: jnp.moveaxis(t, 2, 0)
    S, o = lax.scan(step, s0.astype(jnp.float32), (mv(u), mv(w), mv(attn), mv(q_dec), mv(k_dec), mv(g_last)))
    return jnp.moveaxis(o, 0, 2).reshape(B, H, L, dv), S


def hgrn_chunk_scan(q, kf, logf, i, s0):
    B, H, L, dk = q.shape
    dv = i.shape[-1]
    C = HG_CHUNK
    n = L // C
    chunks = lambda t: jnp.moveaxis(t.reshape(B, H, n, C, t.shape[-1]), 2, 0)
    b = jnp.cumsum(logf.reshape(B, H, n, C, dk), axis=3)
    incl = jnp.tril(jnp.ones((C, C), bool))[:, :, None]

    def step(S, xs):
        q_c, k_c, b_c, i_c = xs
        dec = jnp.exp(jnp.where(incl, b_c[:, :, :, None, :] - b_c[:, :, None, :, :], -jnp.inf))
        attn = jnp.einsum('bhtc,bhjc,bhtjc->bhtj', q_c, k_c, dec)
        b_last = b_c[:, :, -1:, :]
        o = jnp.einsum('bhtc,bhce->bhte', q_c * jnp.exp(b_c), S) + jnp.einsum('bhtj,bhje->bhte', attn, i_c)
        S = jnp.exp(b_last[:, :, 0, :, None]) * S + jnp.einsum('bhjc,bhje->bhce', k_c * jnp.exp(b_last - b_c), i_c)
        return S, o

    S, o = lax.scan(step, s0.astype(jnp.float32), (chunks(q), chunks(kf), jnp.moveaxis(b, 2, 0), chunks(i)))
    return jnp.moveaxis(o, 0, 2).reshape(B, H, L, dv), S


def rwkv_scan(r, dec, k, v, kk, a, s0, reverse):
    def step(S, xs):
        r_t, w_t, k_t, v_t, kk_t, a_t = xs
        sa = jnp.einsum('bhvk,bhk->bhv', S, -kk_t)
        S = S * w_t[:, :, None, :] + sa[..., None] * (kk_t * a_t)[:, :, None, :] + v_t[..., None] * k_t[:, :, None, :]
        return S, jnp.einsum('bhvk,bhk->bhv', S, r_t)

    xs = tuple(jnp.moveaxis(t, 1, 0) for t in (r, dec, k, v, kk, a))
    S, y = lax.scan(step, s0.astype(jnp.float32), xs, reverse=reverse)
    return jnp.moveaxis(y, 0, 1), S


def na_context(q, k, v):
    B, H, L, d = q.shape
    nb = L // Q_BLOCK
    qb = jnp.moveaxis(q.reshape(B, H, nb, Q_BLOCK, d), 2, 0)

    def block(qi):
        s = jnp.einsum('bhqd,bhkd->bhqk', qi, k).astype(jnp.float32)
        p = jax.nn.softmax(s, -1).astype(v.dtype)
        return jnp.einsum('bhqk,bhkd->bhqd', p, v)

    o = lax.map(block, qb)
    return jnp.moveaxis(o, 0, 2).reshape(B, H, L, d)


def na_latent(q, k, v, k_ctx, v_ctx, rpb):
    B, H, L, d = q.shape
    rows = L // GRID_W
    wr = min(NA_WR, rows)
    ncb = GRID_W // NA_QC
    qcol = np.arange(GRID_W).reshape(ncb, NA_QC)
    kc0 = np.clip(np.arange(ncb) * NA_QC - NA_WC // 2, 0, GRID_W - NA_KC)
    kcol = kc0[:, None] + np.arange(NA_KC)
    qs = np.clip(qcol - NA_WC // 2, 0, GRID_W - NA_WC)
    col_ok = (kcol[:, None, :] >= qs[..., None]) & (kcol[:, None, :] < qs[..., None] + NA_WC)
    col_bias = np.clip(kcol[:, None, :] - qcol[..., None] + NA_WC - 1, 0, 2 * NA_WC - 2)
    kg = k.reshape(B, H, rows, GRID_W, d)[:, :, :, kcol]
    vg = v.reshape(B, H, rows, GRID_W, d)[:, :, :, kcol]
    qr = jnp.moveaxis(q.reshape(B, H, rows, ncb, NA_QC, d), 2, 0)
    rpb = rpb.astype(jnp.float32)
    nwin = wr * NA_KC

    def row(xs):
        q_r, r = xs
        s0 = jnp.clip(r - wr // 2, 0, rows - wr)
        k_b = lax.dynamic_slice_in_dim(kg, s0, wr, axis=2)
        v_b = lax.dynamic_slice_in_dim(vg, s0, wr, axis=2)
        s_win = jnp.einsum('bhnqd,bhwnkd->bhnqwk', q_r, k_b).astype(jnp.float32)
        roff = s0 + jnp.arange(wr) - r + NA_WR - 1
        bias = jnp.transpose(rpb[:, roff][:, :, col_bias], (0, 2, 3, 1, 4))
        s_win = jnp.where(col_ok[:, :, None, :], s_win + bias, -jnp.inf)
        s_ctx = jnp.einsum('bhnqd,bhkd->bhnqk', q_r, k_ctx).astype(jnp.float32)
        s = jnp.concatenate([s_win.reshape(B, H, ncb, NA_QC, nwin), s_ctx], -1)
        p = jax.nn.softmax(s, -1).astype(v.dtype)
        p_win = p[..., :nwin].reshape(B, H, ncb, NA_QC, wr, NA_KC)
        return (jnp.einsum('bhnqwk,bhwnkd->bhnqd', p_win, v_b)
                + jnp.einsum('bhnqk,bhkd->bhnqd', p[..., nwin:], v_ctx))

    o = lax.map(row, (qr, jnp.arange(rows)))
    return jnp.moveaxis(o, 0, 2).reshape(B, H, L, d)


def token_mixing(h, lp, lb, cache):
    latent = cache is not None
    B, L, _ = h.shape
    dt = h.dtype
    f32 = jnp.float32
    P = split_in(mm3(h, lp['w_in']))
    if latent:
        s_gdn, s_hg, s_rw, k_ctx, v_ctx = cache
    else:
        s_gdn = jnp.zeros((B, 2, GD_H, GD_DK, GD_DV), f32)
        s_hg = jnp.zeros((B, 2, HG_H, HG_DK, HG_DV), f32)
        s_rw = jnp.zeros((B, 2, RW_H, RW_D, RW_D), f32)

    qkv = jax.nn.silu(centred_dwconv(P['gdn_qkv'], lp['gdn_conv']))
    q, k, v = jnp.split(qkv, [GD_H * GD_DK, 2 * GD_H * GD_DK], axis=-1)
    q = l2_normalize(to_heads(q, GD_H))
    k = l2_normalize(to_heads(k, GD_H))
    if latent:
        q, k = axial_rope(q), axial_rope(k)
    q = q * GD_DK ** -0.5
    v = to_heads(v, GD_H).astype(f32)
    a_in = P['gdn_a'].astype(f32).reshape(B, L, 2, GD_H)
    log_a = jnp.moveaxis(-jnp.exp(lp['gdn_A_log'].astype(f32)) * jax.nn.softplus(a_in + lp['gdn_dt_bias']), 1, -1)
    beta = jnp.moveaxis(jax.nn.sigmoid(P['gdn_b'].astype(f32).reshape(B, L, 2, GD_H)), 1, -1)
    o_f, sg_f = gdn_chunk_scan(q, k, v, log_a[:, 0], beta[:, 0], s_gdn[:, 0])
    o_b, sg_b = gdn_chunk_scan(flip_seq(q), flip_seq(k), flip_seq(v), flip_seq(log_a[:, 1]),
                               flip_seq(beta[:, 1]), s_gdn[:, 1])
    o = o_f + flip_seq(o_b)
    y_a = from_heads(rms_norm(o, lp['gdn_norm']) * jax.nn.silu(to_heads(P['gdn_g'], GD_H).astype(f32))).astype(dt)

    qn = to_heads(P['na_q'], NA_H) * NA_D ** -0.5
    kn = to_heads(P['na_k'], NA_H)
    vn = to_heads(P['na_v'], NA_H)
    on = na_latent(qn, kn, vn, k_ctx, v_ctx, lp['na_rpb']) if latent else na_context(qn, kn, vn)
    y_b = from_heads(on).astype(dt)

    qh = to_heads(P['hg_q'], HG_H).astype(f32)
    ih = to_heads(P['hg_i'], HG_H).astype(f32)
    z = jnp.transpose(P['hg_f'].astype(f32).reshape(B, L, 2, HG_H, HG_DK), (0, 2, 3, 1, 4))
    lbh = lb.reshape(HG_H, 1, HG_DK)
    logf = jnp.logaddexp(jnp.log(lbh), jnp.log1p(-lbh) + jax.nn.log_sigmoid(z))
    kf = (1.0 - lbh) * jax.nn.sigmoid(-z)
    o_f, sh_f = hgrn_chunk_scan(qh, kf[:, 0], logf[:, 0], ih, s_hg[:, 0])
    o_b, sh_b = hgrn_chunk_scan(flip_seq(qh), flip_seq(kf[:, 1]), flip_seq(logf[:, 1]), flip_seq(ih), s_hg[:, 1])
    o = o_f + flip_seq(o_b)
    y_c = from_heads(rms_norm(o, lp['hg_norm']) * jax.nn.silu(to_heads(P['hg_g'], HG_H).astype(f32))).astype(dt)

    xr = P['rw']
    prev, nxt = centred_shift(xr)
    xr = (xr + lp['rw_mu'][0] * (prev - xr) + lp['rw_mu'][1] * (nxt - xr)).astype(f32)
    RW = RW_H * RW_D
    r, kr, vr = xr[..., :RW], xr[..., RW:2 * RW], xr[..., 2 * RW:3 * RW]
    o0 = 3 * RW
    wl = xr[..., o0:o0 + 2 * RW_DECAY_LORA]
    o0 += 2 * RW_DECAY_LORA
    al = xr[..., o0:o0 + RW_AAA_LORA]
    o0 += RW_AAA_LORA
    gl = xr[..., o0:o0 + RW_GATE_LORA]
    a = jax.nn.sigmoid(lp['rw_a0'] + al @ lp['rw_a2'])
    gate = jax.nn.sigmoid(gl) @ lp['rw_g2']
    sp = lambda t: t.reshape(B, L, RW_H, RW_D)
    kk = l2_normalize(sp(kr * lp['rw_kk']))
    kr = kr * (1.0 + (a - 1.0) * lp['rw_ka'])
    r4, k4, v4, a4 = sp(r), sp(kr), sp(vr), sp(a)

    def decay(dirn):
        wd = wl[..., dirn * RW_DECAY_LORA:(dirn + 1) * RW_DECAY_LORA]
        wlog = -jax.nn.softplus(-(lp['rw_w0'][dirn] + jnp.tanh(wd) @ lp['rw_w2'][dirn])) - 0.5
        return sp(jnp.exp(-jnp.exp(wlog)))

    yr_f, sr_f = rwkv_scan(r4, decay(0), k4, v4, kk, a4, s_rw[:, 0], False)
    yr_b, sr_b = rwkv_scan(r4, decay(1), k4, v4, kk, a4, s_rw[:, 1], True)
    y = yr_f + yr_b
    mu = jnp.mean(y, -1, keepdims=True)
    var = jnp.mean(jnp.square(y - mu), -1, keepdims=True)
    y = ((y - mu) * lax.rsqrt(var + RW_GN_EPS)).reshape(B, L, RW) * lp['rw_gn_w'] + lp['rw_gn_b']
    y = y + (jnp.sum(r4 * k4 * lp['rw_rk'], -1, keepdims=True) * v4).reshape(B, L, RW)
    y_d = (y * gate).astype(dt)

    gates = jax.nn.sigmoid(P['merge'].astype(f32)).reshape(B, L, N_BRANCH, D_MODEL)
    branches = [y_a, y_b, y_c, y_d]
    acc = 0.0
    for m in range(N_BRANCH):
        acc = acc + gates[:, :, m] * mm3(branches[m], lp['w_branch'][m])
    out = mm3(acc.astype(dt), lp['w_out'])
    if latent:
        return out, None
    return out, (jnp.stack([sg_f, sg_b], 1), jnp.stack([sh_f, sh_b], 1), jnp.stack([sr_f, sr_b], 1), kn, vn)


def peer(h, wq, keys, u_tab, v_tab):
    B, L, D = h.shape
    q_all = matmul(h.reshape(B * L, D), wq)
    xt = h.reshape(-1, PEER_BLOCK, D)
    qt = q_all.reshape(-1, PEER_BLOCK, PEER_HEADS * PEER_DK)
    half = PEER_DK // 2
    keys = keys.astype(jnp.float32)

    def block(xs):
        xb, qb = xs
        t = xb.shape[0]
        qh = qb.reshape(t, PEER_HEADS, 2, half).astype(jnp.float32)
        s = jnp.einsum('thpd,hpnd->thpn', qh, keys)
        sv, si = lax.top_k(s, PEER_TOPK)
        cand = (sv[:, :, 0, :, None] + sv[:, :, 1, None, :]).reshape(t, PEER_HEADS, -1)
        cidx = (si[:, :, 0, :, None] * PEER_NKEYS + si[:, :, 1, None, :]).reshape(t, PEER_HEADS, -1)
        best, pos = lax.top_k(cand, PEER_TOPK)
        e = jnp.take_along_axis(cidx, pos, -1)
        g = jax.nn.softmax(best, -1)
        act = jax.nn.gelu(jnp.einsum('thkd,td->thk', u_tab[e], xb).astype(jnp.float32))
        return jnp.einsum('thk,thkd->td', (g * act).astype(xb.dtype), v_tab[e])

    return lax.map(block, (xt, qt)).reshape(B, L, D)


def trunk_layer(x, mod, lp, lb, cache, alpha):
    sh1, sc1, g1, sh2, sc2, g2 = mod
    h = x * (1 + sc1) + sh1
    y, ctx_out = token_mixing(h, lp, lb, cache)
    x = layer_norm(alpha * x + g1 * y, lp['ln1_g'], lp['ln1_b'])
    h = x * (1 + sc2) + sh2
    y = peer(h, lp['peer_wq'], lp['peer_keys'], lp['peer_u'], lp['peer_v'])
    x = layer_norm(alpha * x + g2 * y, lp['ln2_g'], lp['ln2_b'])
    return x, ctx_out


def kernel(x_prompt, x_sample, c, c_ctx, state_gdn, state_hgrn, state_rwkv, cache_k_na, cache_v_na,
           w_mod, b_mod, w_in, gdn_conv, gdn_A_log, gdn_dt_bias, gdn_norm, na_rpb, hg_lb_logits, hg_norm,
           rw_mu, rw_w0, rw_w2, rw_a0, rw_a2, rw_g2, rw_kk, rw_ka, rw_rk, rw_gn_w, rw_gn_b,
           w_branch, w_out, ln1_g, ln1_b, ln2_g, ln2_b, peer_wq, peer_keys, peer_u, peer_v):
    alpha = (2.0 * DEPTH) ** 0.25
    lb_all = jnp.cumsum(jax.nn.softmax(hg_lb_logits.astype(jnp.float32), axis=0), axis=0)
    lb_all = lb_all - lb_all[0]
    y_prompt, y_sample = x_prompt, x_sample
    gdn_l, hg_l, rw_l, k_l, v_l = [], [], [], [], []
    for l in range(DEPTH):
        lp = {
            'w_in': w_in[l], 'gdn_conv': gdn_conv[l], 'gdn_A_log': gdn_A_log[l], 'gdn_dt_bias': gdn_dt_bias[l],
            'gdn_norm': gdn_norm[l], 'na_rpb': na_rpb[l], 'hg_norm': hg_norm[l], 'rw_mu': rw_mu[l],
            'rw_w0': rw_w0[l], 'rw_w2': rw_w2[l], 'rw_a0': rw_a0[l], 'rw_a2': rw_a2[l], 'rw_g2': rw_g2[l],
            'rw_kk': rw_kk[l], 'rw_ka': rw_ka[l], 'rw_rk': rw_rk[l], 'rw_gn_w': rw_gn_w[l], 'rw_gn_b': rw_gn_b[l],
            'w_branch': w_branch[l], 'w_out': w_out[l], 'ln1_g': ln1_g[l], 'ln1_b': ln1_b[l],
            'ln2_g': ln2_g[l], 'ln2_b': ln2_b[l], 'peer_wq': peer_wq[l], 'peer_keys': peer_keys[l],
            'peer_u': peer_u[l], 'peer_v': peer_v[l],
        }
        mod_ctx = jnp.split(jax.nn.silu(c_ctx) @ w_mod[l] + b_mod[l], 6, axis=-1)
        mod_lat = [m[:, None, :] for m in jnp.split(jax.nn.silu(c) @ w_mod[l] + b_mod[l], 6, axis=-1)]
        y_prompt, ctx_out = trunk_layer(y_prompt, mod_ctx, lp, lb_all[l], None, alpha)
        cache = (state_gdn[:, l], state_hgrn[:, l], state_rwkv[:, l], cache_k_na[:, l], cache_v_na[:, l])
        y_sample, _ = trunk_layer(y_sample, mod_lat, lp, lb_all[l], cache, alpha)
        gdn_l.append(ctx_out[0])
        hg_l.append(ctx_out[1])
        rw_l.append(ctx_out[2])
        k_l.append(ctx_out[3])
        v_l.append(ctx_out[4])
    return (y_prompt, y_sample, jnp.stack(gdn_l, 1), jnp.stack(hg_l, 1), jnp.stack(rw_l, 1),
            jnp.stack(k_l, 1), jnp.stack(v_l, 1))
```

```python
import functools

import jax
import jax.numpy as jnp
import numpy as np
from jax import lax
from jax.experimental import pallas as pl
from jax.experimental.pallas import tpu as pltpu

D_MODEL = 1024
BATCH = 16
SEQ = 256
DEPTH = 2
DEC_BATCH = 2
DEC_SEQ = 4096
PAST_LEN = 512
GRID_W = 64
GD_H = 4
GD_DK = 128
GD_DV = 128
GD_CHUNK = 64
CONV_K = 3
ROPE_BASE = 10000.0
NA_H = 8
NA_D = 64
NA_WR = 8
NA_WC = 16
NA_QC = 16
NA_KC = 32
Q_BLOCK = 128
HG_H = 4
HG_DK = 128
HG_DV = 128
HG_CHUNK = 32
RW_H = 8
RW_D = 64
RW_DECAY_LORA = 64
RW_AAA_LORA = 64
RW_GATE_LORA = 128
RW_GN_EPS = 64e-5
RW_IN_W = 3 * RW_H * RW_D + 2 * RW_DECAY_LORA + RW_AAA_LORA + RW_GATE_LORA
N_BRANCH = 4
BRANCH_W = 512
PEER_HEADS = 8
PEER_NKEYS = 128
PEER_DK = 256
PEER_TOPK = 16
PEER_BLOCK = 128
N_EXPERTS = PEER_NKEYS * PEER_NKEYS
LN_EPS = 1e-5
NORM_EPS = 1e-6

IN_LAYOUT = (
    ('gdn_qkv', 2 * GD_H * GD_DK + GD_H * GD_DV),
    ('gdn_g', GD_H * GD_DV),
    ('gdn_a', 2 * GD_H),
    ('gdn_b', 2 * GD_H),
    ('na_q', NA_H * NA_D),
    ('na_k', NA_H * NA_D),
    ('na_v', NA_H * NA_D),
    ('hg_q', HG_H * HG_DK),
    ('hg_f', 2 * HG_H * HG_DK),
    ('hg_i', HG_H * HG_DV),
    ('hg_g', HG_H * HG_DV),
    ('rw', RW_IN_W),
    ('merge', N_BRANCH * D_MODEL),
)
IN_W = sum(w for _, w in IN_LAYOUT)

LANES = 128
SUBLANES = 8
VMEM_LIMIT_BYTES = 48 * 1024 * 1024


def _mm_body(x_ref, w_ref, o_ref):
    o_ref[...] = jnp.dot(x_ref[...].astype(jnp.bfloat16), w_ref[...].astype(jnp.bfloat16),
                         preferred_element_type=jnp.float32)


def _pick_tile(n, target):
    t = min(n, target)
    while n % t:
        t -= LANES
    return t


def matmul(x, w, tm=512, tn=1024):
    M, K = x.shape
    N = w.shape[1]
    n_pad = (-N) % LANES
    if n_pad:
        w = jnp.pad(w, ((0, 0), (0, n_pad)))
    Np = N + n_pad
    tm = _pick_tile(M, tm)
    tn = _pick_tile(Np, tn)
    out = pl.pallas_call(
        _mm_body,
        grid=(Np // tn, M // tm),
        in_specs=[pl.BlockSpec((tm, K), lambda j, i: (i, 0)),
                  pl.BlockSpec((K, tn), lambda j, i: (0, j))],
        out_specs=pl.BlockSpec((tm, tn), lambda j, i: (i, j)),
        out_shape=jax.ShapeDtypeStruct((M, Np), jnp.float32),
        compiler_params=pltpu.CompilerParams(dimension_semantics=("arbitrary", "arbitrary"),
                                             vmem_limit_bytes=VMEM_LIMIT_BYTES),
        name="dense_proj",
    )(x, w)
    return out[:, :N] if n_pad else out


def mm3(x, w):
    B, L, K = x.shape
    return matmul(x.reshape(B * L, K), w).reshape(B, L, w.shape[1])


PEER_RT = 256
PEER_HK = PEER_HEADS * PEER_TOPK


def _topk_rows(x, k):
    R = x.shape[0]
    rowi = lax.broadcasted_iota(jnp.int32, x.shape, 0).astype(jnp.float32)
    vals, idxs = [], []
    for _ in range(k):
        m = jnp.max(x, axis=0, keepdims=True)
        idx = jnp.min(jnp.where(x == m, rowi, float(R)), axis=0, keepdims=True)
        vals.append(m)
        idxs.append(idx)
        x = jnp.where(rowi == idx, -jnp.inf, x)
    return jnp.concatenate(vals, axis=0), jnp.concatenate(idxs, axis=0)


def _select_rows(table, idx):
    R = table.shape[0]
    rowi = lax.broadcasted_iota(jnp.int32, table.shape, 0).astype(jnp.float32)
    outs = [jnp.sum(jnp.where(rowi == idx[j:j + 1], table, 0.0), axis=0, keepdims=True)
            for j in range(idx.shape[0])]
    return jnp.concatenate(outs, axis=0)


def _peer_route_body(h_ref, wqt_ref, keys_ref, i1_ref, i2_ref, g_ref):
    hb = h_ref[...].astype(jnp.bfloat16)
    qt = lax.dot_general(wqt_ref[...], hb, (((1,), (1,)), ((), ())),
                         preferred_element_type=jnp.float32)
    half = PEER_DK // 2
    i1_rows, i2_rows, g_rows = [], [], []
    for hd in range(PEER_HEADS):
        sv, si = [], []
        for p in range(2):
            grp = hd * 2 + p
            qg = qt[grp * half:(grp + 1) * half, :].astype(jnp.bfloat16)
            s = jnp.dot(keys_ref[grp], qg, preferred_element_type=jnp.float32)
            v, i = _topk_rows(s, PEER_TOPK)
            sv.append(v)
            si.append(i)
        cand = jnp.concatenate([sv[0][i:i + 1] + sv[1] for i in range(PEER_TOPK)], axis=0)
        best, pos = _topk_rows(cand, PEER_TOPK)
        pi = jnp.floor(pos * (1.0 / PEER_TOPK))
        pj = pos - pi * PEER_TOPK
        i1_rows.append(_select_rows(si[0], pi))
        i2_rows.append(_select_rows(si[1], pj))
        e = jnp.exp(best - jnp.max(best, axis=0, keepdims=True))
        g_rows.append(e / jnp.sum(e, axis=0, keepdims=True))
    i1_ref[...] = jnp.concatenate(i1_rows, axis=0).T
    i2_ref[...] = jnp.concatenate(i2_rows, axis=0).T
    g_ref[...] = jnp.concatenate(g_rows, axis=0).T


def peer_route(h, wqt_bf, keys_bf):
    T, D = h.shape
    tt = PEER_RT
    spec_o = pl.BlockSpec((tt, PEER_HK), lambda i: (i, 0))
    shp = jax.ShapeDtypeStruct((T, PEER_HK), jnp.float32)
    return pl.pallas_call(
        _peer_route_body,
        grid=(T // tt,),
        in_specs=[pl.BlockSpec((tt, D), lambda i: (i, 0)),
                  pl.BlockSpec(wqt_bf.shape, lambda i: (0, 0)),
                  pl.BlockSpec(keys_bf.shape, lambda i: (0, 0, 0))],
        out_specs=[spec_o, spec_o, spec_o],
        out_shape=[shp, shp, shp],
        compiler_params=pltpu.CompilerParams(dimension_semantics=("arbitrary",),
                                             vmem_limit_bytes=VMEM_LIMIT_BYTES),
        name="peer_route",
    )(h, wqt_bf, keys_bf)


PEER_TT = 256
PEER_EB = 512
PEER_W_UNROLL = 4


def _gelu_tanh(x):
    return 0.5 * x * (1.0 + jnp.tanh(0.7978845608028654 * (x + 0.044715 * x * x * x)))


def _peer_expert_body(h_ref, i1_ref, i2_ref, g_ref, u_ref, v_ref, o_ref, hb_ref, w3_ref, acc_ref):
    eb = pl.program_id(1)
    n_eb = pl.num_programs(1)
    tt = PEER_TT
    nk = PEER_NKEYS

    @pl.when(eb == 0)
    def _():
        hb_ref[...] = h_ref[...].astype(jnp.bfloat16)
        acc_ref[...] = jnp.zeros_like(acc_ref)
        sub = lax.broadcasted_iota(jnp.int32, (nk, PEER_HK), 0).astype(jnp.float32)

        def per_token(t, carry):
            i1 = i1_ref[pl.ds(t, 1), :]
            i2 = i2_ref[pl.ds(t, 1), :]
            g = g_ref[pl.ds(t, 1), :]
            g_hi = g.astype(jnp.bfloat16).astype(jnp.float32)
            m1 = sub == i1
            lhs = jnp.concatenate([jnp.where(m1, g_hi, 0.0).astype(jnp.bfloat16),
                                   jnp.where(m1, g - g_hi, 0.0).astype(jnp.bfloat16)], axis=1)
            q = jnp.where(sub == i2, 1.0, 0.0).astype(jnp.bfloat16)
            rhs = jnp.concatenate([q, q], axis=1)
            w = lax.dot_general(lhs, rhs, (((1,), (1,)), ((), ())), preferred_element_type=jnp.float32)
            w3_ref[pl.ds(pl.multiple_of(t * nk, nk), nk), :] = w
            return carry

        lax.fori_loop(0, tt, per_token, 0, unroll=PEER_W_UNROLL)

    act = lax.dot_general(hb_ref[...], u_ref[...], (((1,), (1,)), ((), ())),
                          preferred_element_type=jnp.float32)
    rows_per_step = PEER_EB // nk
    w = jnp.concatenate([w3_ref[pl.ds(eb * rows_per_step + r, tt, stride=nk), :]
                         for r in range(rows_per_step)], axis=1)
    m = (w * _gelu_tanh(act)).astype(jnp.bfloat16)
    acc_ref[...] += jnp.dot(m, v_ref[...], preferred_element_type=jnp.float32)

    @pl.when(eb == n_eb - 1)
    def _():
        o_ref[...] = acc_ref[...]


def peer_experts(h, i1, i2, g, u_bf, v_bf):
    T, D = h.shape
    tt = PEER_TT
    n_eb = N_EXPERTS // PEER_EB
    tok = lambda i, e: (i, 0)
    return pl.pallas_call(
        _peer_expert_body,
        grid=(T // tt, n_eb),
        in_specs=[pl.BlockSpec((tt, D), tok),
                  pl.BlockSpec((tt, PEER_HK), tok),
                  pl.BlockSpec((tt, PEER_HK), tok),
                  pl.BlockSpec((tt, PEER_HK), tok),
                  pl.BlockSpec((PEER_EB, D), lambda i, e: (e, 0)),
                  pl.BlockSpec((PEER_EB, D), lambda i, e: (e, 0))],
        out_specs=pl.BlockSpec((tt, D), tok),
        out_shape=jax.ShapeDtypeStruct((T, D), jnp.float32),
        scratch_shapes=[pltpu.VMEM((tt, D), jnp.bfloat16),
                        pltpu.VMEM((tt * PEER_NKEYS, PEER_NKEYS), jnp.float32),
                        pltpu.VMEM((tt, D), jnp.float32)],
        compiler_params=pltpu.CompilerParams(dimension_semantics=("arbitrary", "arbitrary"),
                                             vmem_limit_bytes=VMEM_LIMIT_BYTES),
        name="peer_experts",
    )(h, i1, i2, g, u_bf, v_bf)


def peer_pallas(h, wqt_bf, keys_bf, u_bf, v_bf):
    i1, i2, g = peer_route(h, wqt_bf, keys_bf)
    return peer_experts(h, i1, i2, g, u_bf, v_bf)


RW_CHUNK = 64


def _split2(x):
    hi = x.astype(jnp.bfloat16)
    lo = (x - hi.astype(jnp.float32)).astype(jnp.bfloat16)
    return hi, lo


def _dot3(x, y):
    xh, xl = _split2(x)
    yh, yl = _split2(y)
    return jnp.dot(jnp.concatenate([xh, xl, xh], axis=1), jnp.concatenate([yh, yh, yl], axis=0),
                   preferred_element_type=jnp.float32)


def _dot3_nt(x, y):
    xh, xl = _split2(x)
    yh, yl = _split2(y)
    return lax.dot_general(jnp.concatenate([xh, xl, xh], axis=1), jnp.concatenate([yh, yh, yl], axis=1),
                           (((1,), (1,)), ((), ())), preferred_element_type=jnp.float32)


def _tri_masks(C, rev):
    row = lax.broadcasted_iota(jnp.int32, (C, C), 0)
    col = lax.broadcasted_iota(jnp.int32, (C, C), 1)
    diff = jnp.where(rev, col - row, row - col)
    return diff >= 0, diff > 0


def _cumsum_rows(incl, x):
    tri = jnp.where(incl, 1.0, 0.0).astype(jnp.bfloat16)
    x0 = x.astype(jnp.bfloat16)
    r1 = x - x0.astype(jnp.float32)
    x1 = r1.astype(jnp.bfloat16)
    x2 = (r1 - x1.astype(jnp.float32)).astype(jnp.bfloat16)
    return jnp.dot(jnp.concatenate([tri, tri, tri], axis=1), jnp.concatenate([x0, x1, x2], axis=0),
                   preferred_element_type=jnp.float32)


def _unit_lower_inverse(n_mat):
    C = n_mat.shape[0]
    eye = jnp.where(lax.broadcasted_iota(jnp.int32, (C, C), 0) == lax.broadcasted_iota(jnp.int32, (C, C), 1), 1.0, 0.0)
    p = eye + n_mat
    q = _dot3(n_mat, n_mat)
    power = 2
    while power < C:
        pq = _dot3(jnp.concatenate([p, q], axis=0), q)
        p = p + pq[:C]
        q = pq[C:]
        power *= 2
    return p


def _rwkv_body(r_ref, k_ref, v_ref, kk_ref, a_ref, lw_ref, h0_ref, y_ref, hT_ref, h_ref):
    d_id = pl.program_id(2)
    c = pl.program_id(3)
    n = pl.num_programs(3)
    C = RW_CHUNK
    D = RW_D

    @pl.when(c == 0)
    def _():
        h_ref[...] = h0_ref[0, 0, 0]

    incl, strict = _tri_masks(C, d_id == 1)
    r = r_ref[0, 0]
    k = k_ref[0, 0]
    v = v_ref[0, 0]
    kk = kk_ref[0, 0]
    a = a_ref[0, 0]
    lw = lw_ref[0, 0, 0]
    cum = _cumsum_rows(incl, lw)
    e_neg = jnp.exp(-cum)
    abar = -kk * jnp.exp(cum - lw)
    bvec = a * kk
    bbar = bvec * e_neg
    kbar = k * e_neg
    rbar = r * jnp.exp(cum)
    g_last = jnp.sum(lw, axis=0, keepdims=True)
    e_tail = jnp.exp(g_last - cum)

    gram = _dot3_nt(jnp.concatenate([abar, rbar], axis=0), jnp.concatenate([bbar, kbar], axis=0))
    l_ab = jnp.where(strict, gram[:C, :C], 0.0)
    l_ak = jnp.where(strict, gram[:C, C:], 0.0)
    a_rb = jnp.where(incl, gram[C:, :C], 0.0)
    a_rk = jnp.where(incl, gram[C:, C:], 0.0)
    t_inv = _unit_lower_inverse(l_ab)
    m12 = _dot3(t_inv, jnp.concatenate([abar, _dot3(l_ak, v)], axis=1))
    y12 = _dot3(a_rb, m12)
    y1 = rbar + y12[:, :D]
    y2 = y12[:, D:] + _dot3(a_rk, v)
    bk_t = jnp.concatenate([bvec * e_tail, k * e_tail], axis=1).T
    bm = _dot3(bk_t[:D], m12)
    kv = _dot3(bk_t[D:], v)
    eye_g = jnp.where(lax.broadcasted_iota(jnp.int32, (D, D), 0) == lax.broadcasted_iota(jnp.int32, (D, D), 1),
                      jnp.exp(g_last), 0.0)
    h = h_ref[...]
    y_ref[0, 0, 0] = _dot3(y1, h) + y2
    h_new = _dot3(eye_g + bm[:, :D], h) + bm[:, D:] + kv
    h_ref[...] = h_new

    @pl.when(c == n - 1)
    def _():
        hT_ref[0, 0, 0] = h_new


def rwkv_pallas(r, k, v, kk, a, logw, h0):
    B, H, L, d = r.shape
    C = RW_CHUNK
    n = L // C
    chunk = lambda b, h, dd, c: (b, h, c + dd * (n - 1 - 2 * c), 0)
    tok_spec = pl.BlockSpec((1, 1, C, d), chunk)
    st_spec = pl.BlockSpec((1, 1, 1, d, d), lambda b, h, dd, c: (b, dd, h, 0, 0))
    return pl.pallas_call(
        _rwkv_body,
        grid=(B, H, 2, n),
        in_specs=[tok_spec] * 5 + [
            pl.BlockSpec((1, 1, 1, C, d), lambda b, h, dd, c: (dd, b, h, c + dd * (n - 1 - 2 * c), 0)),
            st_spec],
        out_specs=[pl.BlockSpec((1, 1, 1, C, d), lambda b, h, dd, c: (dd, b, h, c + dd * (n - 1 - 2 * c), 0)),
                   st_spec],
        out_shape=[jax.ShapeDtypeStruct((2, B, H, L, d), jnp.float32),
                   jax.ShapeDtypeStruct((B, 2, H, d, d), jnp.float32)],
        scratch_shapes=[pltpu.VMEM((d, d), jnp.float32)],
        compiler_params=pltpu.CompilerParams(dimension_semantics=("arbitrary",) * 4,
                                             vmem_limit_bytes=VMEM_LIMIT_BYTES),
        name="rwkv_scan",
    )(r, k, v, kk, a, logw, h0)


def layer_norm(x, g, b):
    xf = x.astype(jnp.float32)
    mu = jnp.mean(xf, -1, keepdims=True)
    var = jnp.mean(jnp.square(xf - mu), -1, keepdims=True)
    return ((xf - mu) * lax.rsqrt(var + LN_EPS) * g + b).astype(x.dtype)


def rms_norm(x, w):
    xf = x.astype(jnp.float32)
    return xf * lax.rsqrt(jnp.mean(xf * xf, -1, keepdims=True) + NORM_EPS) * w


def l2_normalize(x):
    xf = x.astype(jnp.float32)
    return xf * lax.rsqrt(jnp.sum(xf * xf, -1, keepdims=True) + NORM_EPS)


def to_heads(t, n):
    B, L, _ = t.shape
    return jnp.moveaxis(t.reshape(B, L, n, -1), 2, 1)


def from_heads(t):
    B, H, L, d = t.shape
    return jnp.moveaxis(t, 1, 2).reshape(B, L, H * d)


def flip_seq(t):
    return jnp.flip(t, axis=2)


def split_in(p):
    out, off = {}, 0
    for name, w in IN_LAYOUT:
        out[name] = p[..., off:off + w]
        off += w
    return out


def centred_dwconv(x, w):
    pad = CONV_K // 2
    L = x.shape[1]
    xp = jnp.pad(x, ((0, 0), (pad, pad), (0, 0)))
    return sum(xp[:, j:j + L] * w[j] for j in range(CONV_K))


def centred_shift(x):
    zero = jnp.zeros_like(x[:, :1])
    return jnp.concatenate([zero, x[:, :-1]], 1), jnp.concatenate([x[:, 1:], zero], 1)


def axial_rope(x):
    L, dh = x.shape[-2], x.shape[-1]
    half = dh // 2
    nf = half // 2
    inv = ROPE_BASE ** (-jnp.arange(nf, dtype=jnp.float32) / nf)
    t = jnp.arange(L)

    def rot(xa, pos):
        ang = pos.astype(jnp.float32)[:, None] * inv[None, :]
        cos, sin = jnp.cos(ang), jnp.sin(ang)
        x1, x2 = xa[..., :nf], xa[..., nf:]
        return jnp.concatenate([x1 * cos - x2 * sin, x1 * sin + x2 * cos], -1)

    return jnp.concatenate([rot(x[..., :half], t // GRID_W), rot(x[..., half:], t % GRID_W)], -1)


def gdn_chunk_scan(q, k, v, log_a, beta, s0):
    B, H, L, dk = q.shape
    dv = v.shape[-1]
    C = GD_CHUNK
    n = L // C
    q = q.reshape(B, H, n, C, dk)
    k = k.reshape(B, H, n, C, dk)
    v = v.reshape(B, H, n, C, dv)
    beta = beta.reshape(B, H, n, C, 1)
    g = jnp.cumsum(log_a.reshape(B, H, n, C), axis=-1)
    incl = jnp.tril(jnp.ones((C, C), bool))
    strict = jnp.tril(jnp.ones((C, C), bool), -1)
    decay = jnp.exp(jnp.where(incl, g[..., :, None] - g[..., None, :], -jnp.inf))
    a_mat = jnp.where(strict, beta * jnp.einsum('bhncd,bhnjd->bhncj', k, k) * decay, 0.0)
    eye = jnp.eye(C, dtype=jnp.float32)
    rhs = jnp.concatenate([beta * v, beta * jnp.exp(g)[..., None] * k], -1)
    sol = lax.linalg.triangular_solve(eye + a_mat, rhs, left_side=True, lower=True, unit_diagonal=True)
    u, w = sol[..., :dv], sol[..., dv:]
    attn = jnp.einsum('bhncd,bhnjd->bhncj', q, k) * decay
    q_dec = q * jnp.exp(g)[..., None]
    g_last = g[..., -1]
    k_dec = k * jnp.exp(g_last[..., None] - g)[..., None]

    def step(S, xs):
        u_c, w_c, attn_c, qd_c, kd_c, gl_c = xs
        v_new = u_c - jnp.einsum('bhcd,bhde->bhce', w_c, S)
        o = jnp.einsum('bhcd,bhde->bhce', qd_c, S) + jnp.einsum('bhcj,bhje->bhce', attn_c, v_new)
        S = S * jnp.exp(gl_c)[..., None, None] + jnp.einsum('bhcd,bhce->bhde', kd_c, v_new)
        return S, o

    mv = lambda t: jnp.moveaxis(t, 2, 0)
    S, o = lax.scan(step, s0.astype(jnp.float32), (mv(u), mv(w), mv(attn), mv(q_dec), mv(k_dec), mv(g_last)))
    return jnp.moveaxis(o, 0, 2).reshape(B, H, L, dv), S


def hgrn_chunk_scan(q, kf, logf, i, s0):
    B, H, L, dk = q.shape
    dv = i.shape[-1]
    C = HG_CHUNK
    n = L // C
    chunks = lambda t: jnp.moveaxis(t.reshape(B, H, n, C, t.shape[-1]), 2, 0)
    b = jnp.cumsum(logf.reshape(B, H, n, C, dk), axis=3)
    incl = jnp.tril(jnp.ones((C, C), bool))[:, :, None]

    def step(S, xs):
        q_c, k_c, b_c, i_c = xs
        dec = jnp.exp(jnp.where(incl, b_c[:, :, :, None, :] - b_c[:, :, None, :, :], -jnp.inf))
        attn = jnp.einsum('bhtc,bhjc,bhtjc->bhtj', q_c, k_c, dec)
        b_last = b_c[:, :, -1:, :]
        o = jnp.einsum('bhtc,bhce->bhte', q_c * jnp.exp(b_c), S) + jnp.einsum('bhtj,bhje->bhte', attn, i_c)
        S = jnp.exp(b_last[:, :, 0, :, None]) * S + jnp.einsum('bhjc,bhje->bhce', k_c * jnp.exp(b_last - b_c), i_c)
        return S, o

    S, o = lax.scan(step, s0.astype(jnp.float32), (chunks(q), chunks(kf), jnp.moveaxis(b, 2, 0), chunks(i)))
    return jnp.moveaxis(o, 0, 2).reshape(B, H, L, dv), S


def na_context(q, k, v):
    B, H, L, d = q.shape
    nb = L // Q_BLOCK
    qb = jnp.moveaxis(q.reshape(B, H, nb, Q_BLOCK, d), 2, 0)

    def block(qi):
        s = jnp.einsum('bhqd,bhkd->bhqk', qi, k).astype(jnp.float32)
        p = jax.nn.softmax(s, -1).astype(v.dtype)
        return jnp.einsum('bhqk,bhkd->bhqd', p, v)

    o = lax.map(block, qb)
    return jnp.moveaxis(o, 0, 2).reshape(B, H, L, d)


def na_latent(q, k, v, k_ctx, v_ctx, rpb):
    B, H, L, d = q.shape
    rows = L // GRID_W
    wr = min(NA_WR, rows)
    ncb = GRID_W // NA_QC
    qcol = np.arange(GRID_W).reshape(ncb, NA_QC)
    kc0 = np.clip(np.arange(ncb) * NA_QC - NA_WC // 2, 0, GRID_W - NA_KC)
    kcol = kc0[:, None] + np.arange(NA_KC)
    qs = np.clip(qcol - NA_WC // 2, 0, GRID_W - NA_WC)
    col_ok = (kcol[:, None, :] >= qs[..., None]) & (kcol[:, None, :] < qs[..., None] + NA_WC)
    col_bias = np.clip(kcol[:, None, :] - qcol[..., None] + NA_WC - 1, 0, 2 * NA_WC - 2)
    kg = k.reshape(B, H, rows, GRID_W, d)[:, :, :, kcol]
    vg = v.reshape(B, H, rows, GRID_W, d)[:, :, :, kcol]
    qr = jnp.moveaxis(q.reshape(B, H, rows, ncb, NA_QC, d), 2, 0)
    rpb = rpb.astype(jnp.float32)
    nwin = wr * NA_KC

    def row(xs):
        q_r, r = xs
        s0 = jnp.clip(r - wr // 2, 0, rows - wr)
        k_b = lax.dynamic_slice_in_dim(kg, s0, wr, axis=2)
        v_b = lax.dynamic_slice_in_dim(vg, s0, wr, axis=2)
        s_win = jnp.einsum('bhnqd,bhwnkd->bhnqwk', q_r, k_b).astype(jnp.float32)
        roff = s0 + jnp.arange(wr) - r + NA_WR - 1
        bias = jnp.transpose(rpb[:, roff][:, :, col_bias], (0, 2, 3, 1, 4))
        s_win = jnp.where(col_ok[:, :, None, :], s_win + bias, -jnp.inf)
        s_ctx = jnp.einsum('bhnqd,bhkd->bhnqk', q_r, k_ctx).astype(jnp.float32)
        s = jnp.concatenate([s_win.reshape(B, H, ncb, NA_QC, nwin), s_ctx], -1)
        p = jax.nn.softmax(s, -1).astype(v.dtype)
        p_win = p[..., :nwin].reshape(B, H, ncb, NA_QC, wr, NA_KC)
        return (jnp.einsum('bhnqwk,bhwnkd->bhnqd', p_win, v_b)
                + jnp.einsum('bhnqk,bhkd->bhnqd', p[..., nwin:], v_ctx))

    o = lax.map(row, (qr, jnp.arange(rows)))
    return jnp.moveaxis(o, 0, 2).reshape(B, H, L, d)


def token_mixing(h, lp, lb, cache):
    latent = cache is not None
    B, L, _ = h.shape
    dt = h.dtype
    f32 = jnp.float32
    P = split_in(mm3(h, lp['w_in']))
    if latent:
        s_gdn, s_hg, s_rw, k_ctx, v_ctx = cache
    else:
        s_gdn = jnp.zeros((B, 2, GD_H, GD_DK, GD_DV), f32)
        s_hg = jnp.zeros((B, 2, HG_H, HG_DK, HG_DV), f32)
        s_rw = jnp.zeros((B, 2, RW_H, RW_D, RW_D), f32)

    qkv = jax.nn.silu(centred_dwconv(P['gdn_qkv'], lp['gdn_conv']))
    q, k, v = jnp.split(qkv, [GD_H * GD_DK, 2 * GD_H * GD_DK], axis=-1)
    q = l2_normalize(to_heads(q, GD_H))
    k = l2_normalize(to_heads(k, GD_H))
    if latent:
        q, k = axial_rope(q), axial_rope(k)
    q = q * GD_DK ** -0.5
    v = to_heads(v, GD_H).astype(f32)
    a_in = P['gdn_a'].astype(f32).reshape(B, L, 2, GD_H)
    log_a = jnp.moveaxis(-jnp.exp(lp['gdn_A_log'].astype(f32)) * jax.nn.softplus(a_in + lp['gdn_dt_bias']), 1, -1)
    beta = jnp.moveaxis(jax.nn.sigmoid(P['gdn_b'].astype(f32).reshape(B, L, 2, GD_H)), 1, -1)
    o_f, sg_f = gdn_chunk_scan(q, k, v, log_a[:, 0], beta[:, 0], s_gdn[:, 0])
    o_b, sg_b = gdn_chunk_scan(flip_seq(q), flip_seq(k), flip_seq(v), flip_seq(log_a[:, 1]),
                               flip_seq(beta[:, 1]), s_gdn[:, 1])
    o = o_f + flip_seq(o_b)
    y_a = from_heads(rms_norm(o, lp['gdn_norm']) * jax.nn.silu(to_heads(P['gdn_g'], GD_H).astype(f32))).astype(dt)

    qn = to_heads(P['na_q'], NA_H) * NA_D ** -0.5
    kn = to_heads(P['na_k'], NA_H)
    vn = to_heads(P['na_v'], NA_H)
    on = na_latent(qn, kn, vn, k_ctx, v_ctx, lp['na_rpb']) if latent else na_context(qn, kn, vn)
    y_b = from_heads(on).astype(dt)

    qh = to_heads(P['hg_q'], HG_H).astype(f32)
    ih = to_heads(P['hg_i'], HG_H).astype(f32)
    z = jnp.transpose(P['hg_f'].astype(f32).reshape(B, L, 2, HG_H, HG_DK), (0, 2, 3, 1, 4))
    lbh = lb.reshape(HG_H, 1, HG_DK)
    logf = jnp.logaddexp(jnp.log(lbh), jnp.log1p(-lbh) + jax.nn.log_sigmoid(z))
    kf = (1.0 - lbh) * jax.nn.sigmoid(-z)
    o_f, sh_f = hgrn_chunk_scan(qh, kf[:, 0], logf[:, 0], ih, s_hg[:, 0])
    o_b, sh_b = hgrn_chunk_scan(flip_seq(qh), flip_seq(kf[:, 1]), flip_seq(logf[:, 1]), flip_seq(ih), s_hg[:, 1])
    o = o_f + flip_seq(o_b)
    y_c = from_heads(rms_norm(o, lp['hg_norm']) * jax.nn.silu(to_heads(P['hg_g'], HG_H).astype(f32))).astype(dt)

    xr = P['rw']
    prev, nxt = centred_shift(xr)
    xr = (xr + lp['rw_mu'][0] * (prev - xr) + lp['rw_mu'][1] * (nxt - xr)).astype(f32)
    RW = RW_H * RW_D
    r, kr, vr = xr[..., :RW], xr[..., RW:2 * RW], xr[..., 2 * RW:3 * RW]
    o0 = 3 * RW
    wl = xr[..., o0:o0 + 2 * RW_DECAY_LORA]
    o0 += 2 * RW_DECAY_LORA
    al = xr[..., o0:o0 + RW_AAA_LORA]
    o0 += RW_AAA_LORA
    gl = xr[..., o0:o0 + RW_GATE_LORA]
    a = jax.nn.sigmoid(lp['rw_a0'] + al @ lp['rw_a2'])
    gate = jax.nn.sigmoid(gl) @ lp['rw_g2']
    sp = lambda t: t.reshape(B, L, RW_H, RW_D)
    kk = l2_normalize(sp(kr * lp['rw_kk']))
    kr = kr * (1.0 + (a - 1.0) * lp['rw_ka'])
    r4, k4, v4, a4 = sp(r), sp(kr), sp(vr), sp(a)

    def log_decay(dirn):
        wd = wl[..., dirn * RW_DECAY_LORA:(dirn + 1) * RW_DECAY_LORA]
        wlog = -jax.nn.softplus(-(lp['rw_w0'][dirn] + jnp.tanh(wd) @ lp['rw_w2'][dirn])) - 0.5
        return sp(-jnp.exp(wlog))

    hm = lambda t: jnp.moveaxis(t, 2, 1)
    logw = jnp.stack([hm(log_decay(0)), hm(log_decay(1))], 0)
    y2, h_fin = rwkv_pallas(hm(r4), hm(k4), hm(v4), hm(kk), hm(a4), logw, jnp.swapaxes(s_rw, -1, -2))
    y = jnp.moveaxis(y2[0] + y2[1], 1, 2)
    s_rw_new = jnp.swapaxes(h_fin, -1, -2)
    mu = jnp.mean(y, -1, keepdims=True)
    var = jnp.mean(jnp.square(y - mu), -1, keepdims=True)
    y = ((y - mu) * lax.rsqrt(var + RW_GN_EPS)).reshape(B, L, RW) * lp['rw_gn_w'] + lp['rw_gn_b']
    y = y + (jnp.sum(r4 * k4 * lp['rw_rk'], -1, keepdims=True) * v4).reshape(B, L, RW)
    y_d = (y * gate).astype(dt)

    gates = jax.nn.sigmoid(P['merge'].astype(f32)).reshape(B, L, N_BRANCH, D_MODEL)
    branches = [y_a, y_b, y_c, y_d]
    acc = 0.0
    for m in range(N_BRANCH):
        acc = acc + gates[:, :, m] * mm3(branches[m], lp['w_branch'][m])
    out = mm3(acc.astype(dt), lp['w_out'])
    if latent:
        return out, None
    return out, (jnp.stack([sg_f, sg_b], 1), jnp.stack([sh_f, sh_b], 1), s_rw_new, kn, vn)


def trunk_layer(x, mod, lp, lb, cache, alpha):
    sh1, sc1, g1, sh2, sc2, g2 = mod
    h = x * (1 + sc1) + sh1
    y, ctx_out = token_mixing(h, lp, lb, cache)
    x = layer_norm(alpha * x + g1 * y, lp['ln1_g'], lp['ln1_b'])
    h = x * (1 + sc2) + sh2
    B, L, D = h.shape
    y = peer_pallas(h.reshape(B * L, D), lp['peer_wqt'], lp['peer_keys'], lp['peer_u'], lp['peer_v']).reshape(B, L, D)
    x = layer_norm(alpha * x + g2 * y, lp['ln2_g'], lp['ln2_b'])
    return x, ctx_out


def kernel(x_prompt, x_sample, c, c_ctx, state_gdn, state_hgrn, state_rwkv, cache_k_na, cache_v_na,
           w_mod, b_mod, w_in, gdn_conv, gdn_A_log, gdn_dt_bias, gdn_norm, na_rpb, hg_lb_logits, hg_norm,
           rw_mu, rw_w0, rw_w2, rw_a0, rw_a2, rw_g2, rw_kk, rw_ka, rw_rk, rw_gn_w, rw_gn_b,
           w_branch, w_out, ln1_g, ln1_b, ln2_g, ln2_b, peer_wq, peer_keys, peer_u, peer_v):
    alpha = (2.0 * DEPTH) ** 0.25
    lb_all = jnp.cumsum(jax.nn.softmax(hg_lb_logits.astype(jnp.float32), axis=0), axis=0)
    lb_all = lb_all - lb_all[0]
    y_prompt, y_sample = x_prompt, x_sample
    gdn_l, hg_l, rw_l, k_l, v_l = [], [], [], [], []
    for l in range(DEPTH):
        lp = {
            'w_in': w_in[l], 'gdn_conv': gdn_conv[l], 'gdn_A_log': gdn_A_log[l], 'gdn_dt_bias': gdn_dt_bias[l],
            'gdn_norm': gdn_norm[l], 'na_rpb': na_rpb[l], 'hg_norm': hg_norm[l], 'rw_mu': rw_mu[l],
            'rw_w0': rw_w0[l], 'rw_w2': rw_w2[l], 'rw_a0': rw_a0[l], 'rw_a2': rw_a2[l], 'rw_g2': rw_g2[l],
            'rw_kk': rw_kk[l], 'rw_ka': rw_ka[l], 'rw_rk': rw_rk[l], 'rw_gn_w': rw_gn_w[l], 'rw_gn_b': rw_gn_b[l],
            'w_branch': w_branch[l], 'w_out': w_out[l], 'ln1_g': ln1_g[l], 'ln1_b': ln1_b[l],
            'ln2_g': ln2_g[l], 'ln2_b': ln2_b[l],
            'peer_wqt': peer_wq[l].T.astype(jnp.bfloat16),
            'peer_keys': peer_keys[l].reshape(2 * PEER_HEADS, PEER_NKEYS, PEER_DK // 2).astype(jnp.bfloat16),
            'peer_u': peer_u[l].astype(jnp.bfloat16), 'peer_v': peer_v[l].astype(jnp.bfloat16),
        }
        mod_ctx = jnp.split(jax.nn.silu(c_ctx) @ w_mod[l] + b_mod[l], 6, axis=-1)
        mod_lat = [m[:, None, :] for m in jnp.split(jax.nn.silu(c) @ w_mod[l] + b_mod[l], 6, axis=-1)]
        y_prompt, ctx_out = trunk_layer(y_prompt, mod_ctx, lp, lb_all[l], None, alpha)
        cache = (state_gdn[:, l], state_hgrn[:, l], state_rwkv[:, l], cache_k_na[:, l], cache_v_na[:, l])
        y_sample, _ = trunk_layer(y_sample, mod_lat, lp, lb_all[l], cache, alpha)
        gdn_l.append(ctx_out[0])
        hg_l.append(ctx_out[1])
        rw_l.append(ctx_out[2])
        k_l.append(ctx_out[3])
        v_l.append(ctx_out[4])
    return (y_prompt, y_sample, jnp.stack(gdn_l, 1), jnp.stack(hg_l, 1), jnp.stack(rw_l, 1),
            jnp.stack(k_l, 1), jnp.stack(v_l, 1))
```
